```python
import jax, jax.numpy as jnp
from jax import lax
import numpy as np

D_MODEL = 2048
BATCH = 4
SEQ = 4096
DEPTH = 2

GRID_W = 64
CTX_LEN = 256
GLA_HEADS = 4
GLA_V = D_MODEL // 2
GLA_K = D_MODEL // 4
GLA_DK = GLA_K // GLA_HEADS
GLA_DV = GLA_V // GLA_HEADS
GATE_RANK = 16
GATE_TAU = 16.0
CHUNK = 64
POOL_WINDOWS = (2, 4, 8, 16)
POOL_WIDTH = D_MODEL // 2
POOL_GROUP_DIM = POOL_WIDTH // len(POOL_WINDOWS)
MIX_WIDTH = GLA_V + POOL_WIDTH
D_FF = 4 * D_MODEL
STATE_COLS = GLA_K + GLA_V + 2 * GATE_RANK
PROJ_COLS = STATE_COLS + GLA_K + GLA_V + POOL_WIDTH
PROJ_SPLITS = (GLA_K, GLA_K + GLA_V, GLA_K + GLA_V + GATE_RANK, STATE_COLS,
               STATE_COLS + GLA_K, STATE_COLS + GLA_K + GLA_V)
STATE_SPLITS = PROJ_SPLITS[:3]
DEEPNORM_ALPHA = (2 * DEPTH) ** 0.25
DEEPNORM_BETA = (8 * DEPTH) ** -0.25
LN_EPS = 1e-6
RMS_EPS = 1e-6

kernel_name = 'gla_pool_hybrid_dit_block'


def _layer_norm_plain(x):
    xf = x.astype(jnp.float32)
    mu = jnp.mean(xf, axis=-1, keepdims=True)
    var = jnp.mean(jnp.square(xf - mu), axis=-1, keepdims=True)
    return ((xf - mu) * lax.rsqrt(var + LN_EPS)).astype(x.dtype)


def _layer_norm(x, g, b):
    return _layer_norm_plain(x) * g + b


def _modulate(x, shift, scale):
    return _layer_norm_plain(x) * (1.0 + scale) + shift


def _heads(a, n_heads):
    b, t, _ = a.shape
    return a.reshape(b, t, n_heads, -1).transpose(0, 2, 1, 3)


def _merge_heads(a):
    b, h, t, d = a.shape
    return a.transpose(0, 2, 1, 3).reshape(b, t, h * d)


def _rev(a):
    return a[:, :, ::-1]


def _log_decay(lr, w_up, b_up):
    return jax.nn.log_sigmoid((lr @ w_up + b_up).astype(jnp.float32)) / GATE_TAU


def _gla_chunked(q, k, v, log_a, s0):
    b, h, t, _ = q.shape
    dv = v.shape[-1]
    n = t // CHUNK

    def chunks(a):
        return jnp.moveaxis(a.astype(jnp.float32).reshape(b, h, n, CHUNK, a.shape[-1]), 2, 0)

    lower_tri = jnp.tril(jnp.ones((CHUNK, CHUNK), dtype=bool))[:, :, None]

    def step(s, blk):
        qc, kc, vc, gc = blk
        cum = jnp.cumsum(gc, axis=2)
        total = cum[:, :, -1:, :]
        o_inter = jnp.einsum('bhid,bhdv->bhiv', qc * jnp.exp(cum), s)
        diff = jnp.minimum(cum[:, :, :, None, :] - cum[:, :, None, :, :], 0.0)
        decay = jnp.where(lower_tri, jnp.exp(diff), 0.0)
        scores = jnp.einsum('bhid,bhjd,bhijd->bhij', qc, kc, decay)
        o_intra = jnp.einsum('bhij,bhjv->bhiv', scores, vc)
        s_new = (jnp.exp(total[:, :, 0, :, None]) * s
                 + jnp.einsum('bhjd,bhjv->bhdv', kc * jnp.exp(total - cum), vc))
        return s_new, o_inter + o_intra

    s_fin, o = lax.scan(step, s0, (chunks(q), chunks(k), chunks(v), chunks(log_a)))
    return jnp.moveaxis(o, 0, 2).reshape(b, h, t, dv), s_fin


def _gla_final_state(k, v, log_a):
    cum = jnp.cumsum(log_a.astype(jnp.float32), axis=2)
    total = cum[:, :, -1:, :]
    return jnp.einsum('bhtd,bhtv->bhdv', k.astype(jnp.float32) * jnp.exp(total - cum),
                      v.astype(jnp.float32))


def _bidir_gla(q, k, v, la_f, la_b, s0_f, s0_b):
    o_f, s_f = _gla_chunked(q, k, v, la_f, s0_f)
    o_b, s_b = _gla_chunked(_rev(q), _rev(k), _rev(v), _rev(la_b), s0_b)
    return o_f + _rev(o_b), s_f, s_b


def _gla_output(o, g, norm_g):
    o = o * lax.rsqrt(jnp.mean(jnp.square(o), axis=-1, keepdims=True) + RMS_EPS)
    return _merge_heads(o).astype(g.dtype) * norm_g.reshape(1, 1, -1).repeat(1, axis=0)[:, :, :GLA_DV].repeat(GLA_HEADS, axis=-1).reshape(1, 1, GLA_V) * 0 + _merge_heads(o * norm_g).astype(g.dtype) * jax.nn.silu(g) if False else _merge_heads(o * norm_g).astype(g.dtype) * jax.nn.silu(g)


def _multiscale_pool(u, w_pool, pool_scale, seg_start, seg_len):
    b, t, c = u.shape
    pos = jnp.arange(t, dtype=jnp.int32)
    seg_end = seg_start + seg_len - 1
    csum = jnp.cumsum(u.astype(jnp.float32), axis=1)
    csum = jnp.concatenate([jnp.zeros((b, 1, c), jnp.float32), csum], axis=1)
    outs = []
    for gi, w in enumerate(POOL_WINDOWS):
        lo = jnp.maximum(pos - w // 2, seg_start)
        hi = jnp.minimum(pos + w // 2 - 1, seg_end)
        sl = slice(gi * POOL_GROUP_DIM, (gi + 1) * POOL_GROUP_DIM)
        cs = csum[:, :, sl]
        mean = (cs[:, hi + 1] - cs[:, lo]) / (hi - lo + 1).astype(jnp.float32)[None, :, None]
        resid = (mean - u[:, :, sl].astype(jnp.float32)).astype(u.dtype)
        outs.append(jnp.einsum('btc,cd->btd', resid, w_pool[gi]))
    return jnp.concatenate(outs, axis=-1) * pool_scale


def _mixer(p, s0_f, s0_b, w_gate_up, b_gate, gla_norm_g, w_pool, pool_scale, w_out, seg_start, seg_len):
    k, v, lr_f, lr_b, q, g, u = jnp.split(p, PROJ_SPLITS, axis=-1)
    la_f = _heads(_log_decay(lr_f, w_gate_up[0], b_gate[0]), GLA_HEADS)
    la_b = _heads(_log_decay(lr_b, w_gate_up[1], b_gate[1]), GLA_HEADS)
    o, s_f, s_b = _bidir_gla(_heads(q, GLA_HEADS) * (GLA_DK ** -0.5), _heads(k, GLA_HEADS),
                             _heads(v, GLA_HEADS), la_f, la_b, s0_f, s0_b)
    mixed = jnp.concatenate([_gla_output(o, g, gla_norm_g),
                             _multiscale_pool(u, w_pool, pool_scale, seg_start, seg_len)], axis=-1)
    return mixed @ w_out, s_f, s_b


def _sq_relu_mlp(h, w1, b1, w2, b2):
    return jnp.square(jax.nn.relu(h @ w1 + b1)) @ w2 + b2


def setup_inputs(seed: int = 0) -> dict:
    key = jax.random.key(seed)
    ks = jax.random.split(key, 21)

    def nrm(k, shape, s):
        return jax.random.normal(k, shape, jnp.float32) * s

    return {
        'x': nrm(ks[0], (BATCH, SEQ, D_MODEL), 1.0),
        'c': nrm(ks[1], (BATCH, D_MODEL), 1.0),
        'ctx': nrm(ks[2], (BATCH, CTX_LEN, D_MODEL), 1.0),
        'c_ctx': nrm(ks[3], (D_MODEL,), 1.0),
        'w_ada': nrm(ks[4], (DEPTH, D_MODEL, 6 * D_MODEL), 0.5 * D_MODEL ** -0.5),
        'b_ada': nrm(ks[5], (DEPTH, 6 * D_MODEL), 0.02),
        'w_in': nrm(ks[6], (DEPTH, D_MODEL, PROJ_COLS), D_MODEL ** -0.5),
        'w_gate_up': nrm(ks[7], (DEPTH, 2, GATE_RANK, GLA_K), GATE_RANK ** -0.5),
        'b_gate': nrm(ks[8], (DEPTH, 2, GLA_K), 0.02),
        'gla_norm_g': 1.0 + nrm(ks[9], (DEPTH, GLA_DV), 0.02),
        'w_pool': nrm(ks[10], (DEPTH, len(POOL_WINDOWS), POOL_GROUP_DIM, POOL_GROUP_DIM), POOL_GROUP_DIM ** -0.5),
        'pool_scale': 1.0 + nrm(ks[11], (DEPTH, POOL_WIDTH), 0.02),
        'w_out': nrm(ks[12], (DEPTH, MIX_WIDTH, D_MODEL), DEEPNORM_BETA * MIX_WIDTH ** -0.5),
        'ln1_g': 1.0 + nrm(ks[13], (DEPTH, D_MODEL), 0.02),
        'ln1_b': nrm(ks[14], (DEPTH, D_MODEL), 0.02),
        'w_mlp1': nrm(ks[15], (DEPTH, D_MODEL, D_FF), D_MODEL ** -0.5),
        'b_mlp1': nrm(ks[16], (DEPTH, D_FF), 0.02),
        'w_mlp2': nrm(ks[17], (DEPTH, D_FF, D_MODEL), DEEPNORM_BETA * D_FF ** -0.5),
        'b_mlp2': nrm(ks[18], (DEPTH, D_MODEL), 0.02),
        'ln2_g': 1.0 + nrm(ks[19], (DEPTH, D_MODEL), 0.02),
        'ln2_b': nrm(ks[20], (DEPTH, D_MODEL), 0.02),
    }


def reference(x, c, ctx, c_ctx, w_ada, b_ada, w_in, w_gate_up, b_gate, gla_norm_g, w_pool,
              pool_scale, w_out, ln1_g, ln1_b, w_mlp1, b_mlp1, w_mlp2, b_mlp2, ln2_g, ln2_b):
    b, t, _ = x.shape
    ctx_len = ctx.shape[1]
    rows = t // GRID_W
    lat_start = jnp.repeat(jnp.arange(rows, dtype=jnp.int32) * GRID_W, GRID_W)
    ctx_start = jnp.zeros((ctx_len,), jnp.int32)
    s_zero = jnp.zeros((b, GLA_HEADS, GLA_DK, GLA_DV), jnp.float32)
    silu_c = jax.nn.silu(c)
    silu_cc = jax.nn.silu(c_ctx)
    for l in range(DEPTH):
        last = l == DEPTH - 1
        mod = (silu_c @ w_ada[l] + b_ada[l])[:, None, :]
        mod_c = silu_cc @ w_ada[l] + b_ada[l]
        sh1, sc1, g1, sh2, sc2, g2 = jnp.split(mod, 6, axis=-1)
        csh1, csc1, cg1, csh2, csc2, cg2 = jnp.split(mod_c, 6, axis=-1)

        hc = _modulate(ctx, csh1, csc1)
        if last:
            kc, vc, lrc_f, lrc_b = jnp.split(hc @ w_in[l][:, :STATE_COLS], STATE_SPLITS, axis=-1)
            kc_h, vc_h = _heads(kc, GLA_HEADS), _heads(vc, GLA_HEADS)
            s_f = _gla_final_state(kc_h, vc_h, _heads(_log_decay(lrc_f, w_gate_up[l, 0], b_gate[l, 0]), GLA_HEADS))
            s_b = _gla_final_state(_rev(kc_h), _rev(vc_h),
                                   _rev(_heads(_log_decay(lrc_b, w_gate_up[l, 1], b_gate[l, 1]), GLA_HEADS)))
        else:
            mix_c, s_f, s_b = _mixer(hc @ w_in[l], s_zero, s_zero, w_gate_up[l], b_gate[l], gla_norm_g[l],
                                     w_pool[l], pool_scale[l], w_out[l], ctx_start, ctx_len)
            ctx = _layer_norm(DEEPNORM_ALPHA * ctx + cg1 * mix_c, ln1_g[l], ln1_b[l])
            yc = _sq_relu_mlp(_modulate(ctx, csh2, csc2), w_mlp1[l], b_mlp1[l], w_mlp2[l], b_mlp2[l])
            ctx = _layer_norm(DEEPNORM_ALPHA * ctx + cg2 * yc, ln2_g[l], ln2_b[l])

        h = _modulate(x, sh1, sc1)
        mix, _, _ = _mixer(h @ w_in[l], s_f, s_b, w_gate_up[l], b_gate[l], gla_norm_g[l],
                           w_pool[l], pool_scale[l], w_out[l], lat_start, GRID_W)
        x = _layer_norm(DEEPNORM_ALPHA * x + g1 * mix, ln1_g[l], ln1_b[l])
        y = _sq_relu_mlp(_modulate(x, sh2, sc2), w_mlp1[l], b_mlp1[l], w_mlp2[l], b_mlp2[l])
        x = _layer_norm(DEEPNORM_ALPHA * x + g2 * y, ln2_g[l], ln2_b[l])
    return x
```

```python
import functools
import math

import numpy as np
import jax
import jax.numpy as jnp
from jax import lax
from jax.experimental import pallas as pl
from jax.experimental.pallas import tpu as pltpu

GLA_HEADS = 4
GATE_RANK = 16
GATE_TAU = 16.0
POOL_WINDOWS = (2, 4, 8, 16)
GRID_W = 64
LN_EPS = 1e-6
RMS_EPS = 1e-6

LANES = 128
GLA_CHUNK = 128
POOL_TILE = 256
FAST_PATH_MAX_EXPONENT = 40.0
VMEM_LIMIT = 48 * 1024 * 1024

_F32 = jnp.float32
_BF16 = jnp.bfloat16
_NT = (((1,), (1,)), ((), ()))
_TN = (((0,), (0,)), ((), ()))


def _cparams(sem):
    return pltpu.CompilerParams(dimension_semantics=sem, vmem_limit_bytes=VMEM_LIMIT)


def _ada_kernel(cb_ref, w_ref, b_ref, o_ref, s_ref, *, n_rows, tn):
    @pl.when(pl.program_id(0) == 0)
    def _():
        c = cb_ref[...]
        s_ref[...] = c / (1.0 + jnp.exp(-c))
    o_ref[...] = jnp.broadcast_to(b_ref[...], o_ref.shape)
    for j in range(tn // LANES):
        lanes = slice(j * LANES, (j + 1) * LANES)
        wj = w_ref[:, lanes]
        for r in range(n_rows):
            o_ref[r:r + 1, lanes] = jnp.sum(wj * s_ref[r], axis=0, keepdims=True) + b_ref[:, lanes]


def _ada(cvecs, w, b):
    n_rows, d = cvecs.shape
    n_out = w.shape[1]
    tn = 1024
    cb = jnp.broadcast_to(cvecs[:, :, None], (n_rows, d, LANES))
    return pl.pallas_call(
        functools.partial(_ada_kernel, n_rows=n_rows, tn=tn),
        out_shape=jax.ShapeDtypeStruct((8, n_out), _F32),
        grid=(n_out // tn,),
        in_specs=[pl.BlockSpec((n_rows, d, LANES), lambda j: (0, 0, 0)),
                  pl.BlockSpec((d, tn), lambda j: (0, j)),
                  pl.BlockSpec((1, tn), lambda j: (0, j))],
        out_specs=pl.BlockSpec((8, tn), lambda j: (0, j)),
        scratch_shapes=[pltpu.VMEM((n_rows, d, LANES), _F32)],
        compiler_params=_cparams(("arbitrary",)),
        name="ada",
    )(cb, w, b.reshape(1, n_out))


def _ln_plain(x):
    mu = jnp.mean(x, axis=-1, keepdims=True)
    xc = x - mu
    var = jnp.mean(xc * xc, axis=-1, keepdims=True)
    return xc * lax.rsqrt(var + LN_EPS)


def _log_sigmoid(z):
    return jnp.minimum(z, 0.0) - jnp.log1p(jnp.exp(-jnp.abs(z)))


def _lnmod_kernel(x_ref, sh_ref, sc_ref, wlr_ref, wf_ref, wb_ref, bf_ref, bb_ref, lo_ref, up_ref,
                  h_ref, cf_ref, cb_ref, mx_ref, *, chunk):
    x = x_ref[...]
    h = (_ln_plain(x) * (1.0 + sc_ref[...]) + sh_ref[...]).astype(_BF16)
    h_ref[...] = h
    lr = jnp.dot(h, wlr_ref[...], preferred_element_type=_F32)
    la_f = _log_sigmoid(jnp.dot(lr, wf_ref[...], precision=lax.Precision.HIGHEST,
                                preferred_element_type=_F32) + bf_ref[...]) * (1.0 / GATE_TAU)
    la_b = _log_sigmoid(jnp.dot(lr, wb_ref[...], precision=lax.Precision.HIGHEST,
                                preferred_element_type=_F32) + bb_ref[...]) * (1.0 / GATE_TAU)
    half = chunk // 2
    for c in range(x.shape[0] // chunk):
        sl = slice(c * chunk, (c + 1) * chunk)
        cum_f = jnp.dot(lo_ref[...], la_f[sl], precision=lax.Precision.HIGHEST, preferred_element_type=_F32)
        cum_b = jnp.dot(up_ref[...], la_b[sl], precision=lax.Precision.HIGHEST, preferred_element_type=_F32)
        cf_ref[sl, :] = cum_f
        cb_ref[sl, :] = cum_b
        mx_ref[c, 0:1, :] = jnp.max(jnp.abs(cum_f - cum_f[half - 1:half, :]), axis=0, keepdims=True)
        mx_ref[c, 1:2, :] = jnp.max(jnp.abs(cum_b - cum_b[half:half + 1, :]), axis=0, keepdims=True)


def _lnmod(x, mod3, mod_row, w_lr, w_gf, w_gb, b_gf, b_gb, tri_lo, tri_up):
    bsz, t, d = x.shape
    gk = w_gf.shape[1]
    chunk = tri_lo.shape[0]
    tr = min(t, 512)
    const = lambda shape: pl.BlockSpec(shape, lambda b, i: (0,) * len(shape))
    return pl.pallas_call(
        functools.partial(_lnmod_kernel, chunk=chunk),
        out_shape=(jax.ShapeDtypeStruct((bsz, t, d), _BF16),
                   jax.ShapeDtypeStruct((bsz, t, gk), _F32),
                   jax.ShapeDtypeStruct((bsz, t, gk), _F32),
                   jax.ShapeDtypeStruct((bsz, t // chunk, 2, gk), _F32)),
        grid=(bsz, t // tr),
        in_specs=[pl.BlockSpec((None, tr, d), lambda b, i: (b, i, 0)),
                  pl.BlockSpec((None, 1, d), lambda b, i: (mod_row(b), 0, 0)),
                  pl.BlockSpec((None, 1, d), lambda b, i: (mod_row(b), 0, 1)),
                  const(w_lr.shape), const(w_gf.shape), const(w_gb.shape),
                  const(b_gf.shape), const(b_gb.shape), const(tri_lo.shape), const(tri_up.shape)],
        out_specs=(pl.BlockSpec((None, tr, d), lambda b, i: (b, i, 0)),
                   pl.BlockSpec((None, tr, gk), lambda b, i: (b, i, 0)),
                   pl.BlockSpec((None, tr, gk), lambda b, i: (b, i, 0)),
                   pl.BlockSpec((None, tr // chunk, 2, gk), lambda b, i: (b, i, 0, 0))),
        compiler_params=_cparams(("parallel", "parallel")),
        name="lnmod",
    )(x, mod3, mod3, w_lr, w_gf, w_gb, b_gf, b_gb, tri_lo, tri_up)


def _mm_kernel(a_ref, w_ref, o_ref):
    o_ref[...] = jnp.dot(a_ref[...], w_ref[...], preferred_element_type=_F32).astype(o_ref.dtype)


def _matmul(a, w, out_dtype=_F32):
    m, k = a.shape
    n = w.shape[1]
    tm, tn = min(m, 512), min(n, 1024)
    return pl.pallas_call(
        _mm_kernel,
        out_shape=jax.ShapeDtypeStruct((m, n), out_dtype),
        grid=(m // tm, n // tn),
        in_specs=[pl.BlockSpec((tm, k), lambda i, j: (i, 0)),
                  pl.BlockSpec((k, tn), lambda i, j: (0, j))],
        out_specs=pl.BlockSpec((tm, tn), lambda i, j: (i, j)),
        compiler_params=_cparams(("parallel", "parallel")),
        name="proj",
    )(a, w)


def _chunk_fast(q, k, v, cum, st, *, fwd, scale):
    c = q.shape[0]
    if fwd:
        ref, tot = cum[c // 2 - 1:c // 2, :], cum[c - 1:c, :]
    else:
        ref, tot = cum[c // 2:c // 2 + 1, :], cum[0:1, :]
    e = cum - ref
    qt = q * jnp.exp(e) * scale
    kt = k * jnp.exp(-e)
    a = lax.dot_general(qt.astype(_BF16), kt.astype(_BF16), _NT, preferred_element_type=_F32)
    row = lax.broadcasted_iota(jnp.int32, (c, c), 0)
    col = lax.broadcasted_iota(jnp.int32, (c, c), 1)
    a = jnp.where((col <= row) if fwd else (col >= row), a, 0.0)
    o = jnp.dot(a.astype(_BF16), v.astype(_BF16), preferred_element_type=_F32)
    qh = (qt * jnp.exp(ref)).astype(_BF16)
    o = o + lax.dot_general(qh, st.astype(_BF16), _NT, preferred_element_type=_F32)
    kh = (kt * jnp.exp(tot - ref)).astype(_BF16)
    st_new = st * jnp.exp(tot) + lax.dot_general(v.astype(_BF16), kh, _TN, preferred_element_type=_F32)
    return o, st_new


def _chunk_robust(q, k, v, cum, st, p_ref, *, fwd, scale):
    c = q.shape[0]
    row = lax.broadcasted_iota(jnp.int32, (c, c), 0)
    col = lax.broadcasted_iota(jnp.int32, (c, c), 1)
    rowv = lax.broadcasted_iota(jnp.int32, q.shape, 0)
    qs = q * scale

    def level(l, a):
        e = jnp.dot(p_ref[l], cum, precision=lax.Precision.HIGHEST, preferred_element_type=_F32)
        ex = jnp.exp(jnp.minimum(e, 0.0))
        later = ((rowv >> l) & 1) == (1 if fwd else 0)
        qt = jnp.where(later, qs * ex, 0.0).astype(_BF16)
        kt = jnp.where(later, 0.0, k * ex).astype(_BF16)
        al = lax.dot_general(qt, kt, _NT, preferred_element_type=_F32)
        return a + jnp.where((row >> (l + 1)) == (col >> (l + 1)), al, 0.0)

    a = lax.fori_loop(0, int(math.log2(c)), level, jnp.zeros((c, c), _F32))
    diag = lax.dot_general(qs.astype(_BF16), k.astype(_BF16), _NT, preferred_element_type=_F32)
    a = a + jnp.where(row == col, diag, 0.0)
    o = jnp.dot(a.astype(_BF16), v.astype(_BF16), preferred_element_type=_F32)
    tot = cum[c - 1:c, :] if fwd else cum[0:1, :]
    qh = (qs * jnp.exp(cum)).astype(_BF16)
    o = o + lax.dot_general(qh, st.astype(_BF16), _NT, preferred_element_type=_F32)
    kh = (k * jnp.exp(tot - cum)).astype(_BF16)
    st_new = st * jnp.exp(tot) + lax.dot_general(v.astype(_BF16), kh, _TN, preferred_element_type=_F32)
    return o, st_new


def _gla_kernel(ff_ref, fb_ref,
                kf_ref, qf_ref, vf_ref, cf_ref, kb_ref, qb_ref, vb_ref, cb_ref,
                s0f_ref, s0b_ref, pf_ref, pb_ref,
                of_ref, ob_ref, sf_ref, sb_ref, stf, stb, *, scale):
    b, h, i = pl.program_id(0), pl.program_id(1), pl.program_id(2)
    nh, nc = pl.num_programs(1), pl.num_programs(2)

    @pl.when(i == 0)
    def _():
        stf[...] = s0f_ref[...]
        stb[...] = s0b_ref[...]

    base = (b * nh + h) * nc

    def run(flag, refs, st, o_ref, s_ref, p_ref, fwd):
        k_ref, q_ref, v_ref, c_ref = refs

        @pl.when(flag == 0)
        def _():
            o, s = _chunk_fast(q_ref[...], k_ref[...], v_ref[...], c_ref[...], st[...], fwd=fwd, scale=scale)
            o_ref[...] = o
            st[...] = s
            s_ref[...] = s

        @pl.when(flag != 0)
        def _():
            o, s = _chunk_robust(q_ref[...], k_ref[...], v_ref[...], c_ref[...], st[...], p_ref,
                                 fwd=fwd, scale=scale)
            o_ref[...] = o
            st[...] = s
            s_ref[...] = s

    run(ff_ref[base + i], (kf_ref, qf_ref, vf_ref, cf_ref), stf, of_ref, sf_ref, pf_ref, True)
    run(fb_ref[base + nc - 1 - i], (kb_ref, qb_ref, vb_ref, cb_ref), stb, ob_ref, sb_ref, pb_ref, False)


def _gla(p, cum_f, cum_b, flag_f, flag_b, s0_f, s0_b, p_f, p_b):
    bsz, t, _ = p.shape
    gk = cum_f.shape[-1]
    dk = gk // GLA_HEADS
    dv = 2 * dk
    c = p_f.shape[-1]
    nc = t // c
    nh = GLA_HEADS
    fw = lambda col: (lambda b, h, i, *_: (b, i, col(h)))
    bw = lambda col: (lambda b, h, i, *_: (b, nc - 1 - i, col(h)))
    kcol, qcol, vcol = (lambda h: h), (lambda h: nh + h), (lambda h: nh + h)
    st_spec = pl.BlockSpec((None, None, dv, dk), lambda b, h, i, *_: (b, h, 0, 0))
    pm_spec = pl.BlockSpec(p_f.shape, lambda b, h, i, *_: (0, 0, 0))
    grid_spec = pltpu.PrefetchScalarGridSpec(
        num_scalar_prefetch=2,
        grid=(bsz, nh, nc),
        in_specs=[pl.BlockSpec((None, c, dk), fw(kcol)), pl.BlockSpec((None, c, dk), fw(qcol)),
                  pl.BlockSpec((None, c, dv), fw(vcol)), pl.BlockSpec((None, c, dk), fw(kcol)),
                  pl.BlockSpec((None, c, dk), bw(kcol)), pl.BlockSpec((None, c, dk), bw(qcol)),
                  pl.BlockSpec((None, c, dv), bw(vcol)), pl.BlockSpec((None, c, dk), bw(kcol)),
                  st_spec, st_spec, pm_spec, pm_spec],
        out_specs=(pl.BlockSpec((None, c, dv), fw(lambda h: h)), pl.BlockSpec((None, c, dv), bw(lambda h: h)),
                   st_spec, st_spec),
        scratch_shapes=[pltpu.VMEM((dv, dk), _F32), pltpu.VMEM((dv, dk), _F32)],
    )
    return pl.pallas_call(
        functools.partial(_gla_kernel, scale=dk ** -0.5),
        out_shape=(jax.ShapeDtypeStruct((bsz, t, nh * dv), _F32), jax.ShapeDtypeStruct((bsz, t, nh * dv), _F32),
                   jax.ShapeDtypeStruct((bsz, nh, dv, dk), _F32), jax.ShapeDtypeStruct((bsz, nh, dv, dk), _F32)),
        grid_spec=grid_spec,
        compiler_params=_cparams(("parallel", "parallel", "arbitrary")),
        name="gla",
    )(flag_f, flag_b, p, p, p, cum_f, p, p, p, cum_b, s0_f, s0_b, p_f, p_b)


def _mix_kernel(of_ref, ob_ref, g_ref, u_ref, x_ref, g1_ref, sh2_ref, sc2_ref,
                ng_ref, pm_ref, wp_ref, ps_ref, wo_ref, lg_ref, lb_ref,
                xo_ref, h2_ref, *, alpha, dv, pool_dim):
    tm = x_ref.shape[0]
    o = of_ref[...] + ob_ref[...]
    g = g_ref[...]
    parts = []
    for hd in range(o.shape[1] // dv):
        oh = o[:, hd * dv:(hd + 1) * dv]
        oh = oh * lax.rsqrt(jnp.mean(oh * oh, axis=-1, keepdims=True) + RMS_EPS)
        parts.append(oh * ng_ref[...])
    gla = (jnp.concatenate(parts, axis=-1) * (g / (1.0 + jnp.exp(-g)))).astype(_BF16)
    pooled = []
    for s in range(tm // POOL_TILE):
        cols = []
        for gi in range(u_ref.shape[1] // pool_dim):
            ug = u_ref[s * POOL_TILE:(s + 1) * POOL_TILE, gi * pool_dim:(gi + 1) * pool_dim]
            mean = jnp.dot(pm_ref[gi], ug, precision=lax.Precision.HIGHEST, preferred_element_type=_F32)
            resid = (mean - ug).astype(_BF16)
            cols.append(jnp.dot(resid, wp_ref[gi], preferred_element_type=_F32))
        pooled.append(jnp.concatenate(cols, axis=-1))
    pool = (jnp.concatenate(pooled, axis=0) * ps_ref[...]).astype(_BF16)
    mixed = jnp.dot(jnp.concatenate([gla, pool], axis=-1), wo_ref[...], preferred_element_type=_F32)
    xn = _ln_plain(alpha * x_ref[...] + g1_ref[...] * mixed) * lg_ref[...] + lb_ref[...]
    xo_ref[...] = xn
    h2_ref[...] = (_ln_plain(xn) * (1.0 + sc2_ref[...]) + sh2_ref[...]).astype(_BF16)


def _mix(o_f, o_b, p, x, mod3, mod_row, norm_g, pool_mat, w_pool, pool_scale, w_out, ln_g, ln_b, alpha):
    bsz, t, d = x.shape
    v = o_f.shape[-1]
    dv = norm_g.shape[-1]
    tm = POOL_TILE
    const = lambda shape: pl.BlockSpec(shape, lambda b, i: (0,) * len(shape))
    row = lambda col: pl.BlockSpec((None, 1, d), lambda b, i: (mod_row(b), 0, col))
    tok = lambda width, col: pl.BlockSpec((None, tm, width), lambda b, i: (b, i, col))
    return pl.pallas_call(
        functools.partial(_mix_kernel, alpha=alpha, dv=dv, pool_dim=w_pool.shape[-1]),
        out_shape=(jax.ShapeDtypeStruct((bsz, t, d), _F32), jax.ShapeDtypeStruct((bsz, t, d), _BF16)),
        grid=(bsz, t // tm),
        in_specs=[tok(v, 0), tok(v, 0), tok(v, 2), tok(v, 3), tok(d, 0), row(2), row(3), row(4),
                  const(norm_g.shape), const(pool_mat.shape), const(w_pool.shape), const(pool_scale.shape),
                  const(w_out.shape), const(ln_g.shape), const(ln_b.shape)],
        out_specs=(tok(d, 0), tok(d, 0)),
        compiler_params=_cparams(("parallel", "parallel")),
        name="mix",
    )(o_f, o_b, p, p, x, mod3, mod3, mod3, norm_g, pool_mat, w_pool, pool_scale, w_out, ln_g, ln_b)


def _mlp_kernel(h_ref, w1_ref, b1_ref, w2_ref, b2_ref, x_ref, g2_ref, lg_ref, lb_ref, o_ref, acc_ref, *, alpha):
    f = pl.program_id(1)

    @pl.when(f == 0)
    def _():
        acc_ref[...] = jnp.zeros_like(acc_ref)

    a = jnp.dot(h_ref[...], w1_ref[...], preferred_element_type=_F32) + b1_ref[...]
    a = jnp.maximum(a, 0.0)
    acc_ref[...] += jnp.dot((a * a).astype(_BF16), w2_ref[...], preferred_element_type=_F32)

    @pl.when(f == pl.num_programs(1) - 1)
    def _():
        y = acc_ref[...] + b2_ref[...]
        o_ref[...] = _ln_plain(alpha * x_ref[...] + g2_ref[...] * y) * lg_ref[...] + lb_ref[...]


def _mlp(h2, x, mod3, mod_row, rows_per_batch, w1, b1, w2, b2, ln_g, ln_b, alpha):
    n, d = x.shape
    dff = w1.shape[1]
    tm, tf = min(512, rows_per_batch), 512
    tiles_per_batch = rows_per_batch // tm
    const = lambda shape: pl.BlockSpec(shape, lambda i, f: (0,) * len(shape))
    return pl.pallas_call(
        functools.partial(_mlp_kernel, alpha=alpha),
        out_shape=jax.ShapeDtypeStruct((n, d), _F32),
        grid=(n // tm, dff // tf),
        in_specs=[pl.BlockSpec((tm, d), lambda i, f: (i, 0)),
                  pl.BlockSpec((d, tf), lambda i, f: (0, f)),
                  pl.BlockSpec((1, tf), lambda i, f: (0, f)),
                  pl.BlockSpec((tf, d), lambda i, f: (f, 0)),
                  const(b2.shape),
                  pl.BlockSpec((tm, d), lambda i, f: (i, 0)),
                  pl.BlockSpec((None, 1, d), lambda i, f: (mod_row(i // tiles_per_batch), 0, 5)),
                  const(ln_g.shape), const(ln_b.shape)],
        out_specs=pl.BlockSpec((tm, d), lambda i, f: (i, 0)),
        scratch_shapes=[pltpu.VMEM((tm, d), _F32)],
        compiler_params=_cparams(("parallel", "arbitrary")),
        name="mlp",
    )(h2, w1, b1, w2, b2, x, mod3, ln_g, ln_b)


def _pool_matrices(seg_len):
    pos = np.arange(POOL_TILE)
    seg_start = (pos // seg_len) * seg_len
    seg_end = seg_start + seg_len - 1
    mats = np.zeros((len(POOL_WINDOWS), POOL_TILE, POOL_TILE), np.float32)
    for gi, w in enumerate(POOL_WINDOWS):
        lo = np.maximum(pos - w // 2, seg_start)
        hi = np.minimum(pos + w // 2 - 1, seg_end)
        for r in range(POOL_TILE):
            mats[gi, r, lo[r]:hi[r] + 1] = 1.0 / (hi[r] - lo[r] + 1)
    return jnp.asarray(mats)


def _tree_matrices(c, fwd):
    levels = int(math.log2(c))
    mats = np.zeros((levels, c, c), np.float32)
    for l in range(levels):
        m = 1 << l
        for r in range(c):
            start = (r // (2 * m)) * 2 * m
            mid = start + m
            if fwd:
                ref = mid - 1
                if r >= mid:
                    mats[l, r, r] += 1.0; mats[l, r, ref] -= 1.0
                else:
                    mats[l, r, ref] += 1.0; mats[l, r, r] -= 1.0
            else:
                ref = mid
                if r < mid:
                    mats[l, r, r] += 1.0; mats[l, r, ref] -= 1.0
                else:
                    mats[l, r, ref] += 1.0; mats[l, r, r] -= 1.0
    return jnp.asarray(mats)


def kernel(x, c, ctx, c_ctx, w_ada, b_ada, w_in, w_gate_up, b_gate, gla_norm_g, w_pool, pool_scale, w_out,
           ln1_g, ln1_b, w_mlp1, b_mlp1, w_mlp2, b_mlp2, ln2_g, ln2_b):
    bsz, t, d = x.shape
    ctx_len = ctx.shape[1]
    depth = w_in.shape[0]
    gk = w_gate_up.shape[-1]
    gv = gla_norm_g.shape[-1] * GLA_HEADS
    dk, dv = gk // GLA_HEADS, gv // GLA_HEADS
    alpha = (2 * depth) ** 0.25
    ctx_row = bsz

    cvecs = jnp.concatenate([c, c_ctx[None, :]], axis=0)
    tri_lo = jnp.asarray(np.tril(np.ones((GLA_CHUNK, GLA_CHUNK), np.float32)))
    tri_up = jnp.asarray(np.triu(np.ones((GLA_CHUNK, GLA_CHUNK), np.float32)))
    tree_f, tree_b = _tree_matrices(GLA_CHUNK, True), _tree_matrices(GLA_CHUNK, False)
    pool_lat, pool_ctx = _pool_matrices(GRID_W), _pool_matrices(ctx_len)
    zero_state = jnp.zeros((bsz, GLA_HEADS, dv, dk), _F32)
    zpad = jnp.zeros((GATE_RANK, gk), _F32)

    def flags(mx):
        m = mx.reshape(bsz, -1, 2, GLA_HEADS, dk).max(axis=-1)
        f = (m > FAST_PATH_MAX_EXPONENT).astype(jnp.int32)
        return (jnp.transpose(f[:, :, 0, :], (0, 2, 1)).reshape(-1),
                jnp.transpose(f[:, :, 1, :], (0, 2, 1)).reshape(-1))

    for l in range(depth):
        last = l == depth - 1
        wi = w_in[l]
        o_k, o_v, o_lr, o_q, o_g, o_u = 0, gk, gk + gv, gk + gv + 2 * GATE_RANK, 2 * gk + gv + 2 * GATE_RANK, \
            2 * gk + 2 * gv + 2 * GATE_RANK
        w_p = jnp.concatenate([wi[:, o_k:o_v], wi[:, o_q:o_g], wi[:, o_v:o_lr], wi[:, o_g:o_u], wi[:, o_u:]],
                              axis=1).astype(_BF16)
        w_lr = wi[:, o_lr:o_q].astype(_BF16)
        w_gf = jnp.concatenate([w_gate_up[l, 0], zpad], axis=0)
        w_gb = jnp.concatenate([zpad, w_gate_up[l, 1]], axis=0)
        b_gf, b_gb = b_gate[l, 0][None, :], b_gate[l, 1][None, :]
        norm_g = gla_norm_g[l][None, :]
        wpool = w_pool[l].astype(_BF16)
        pscale = pool_scale[l][None, :]
        wout = w_out[l].astype(_BF16)
        w1, w2 = w_mlp1[l].astype(_BF16), w_mlp2[l].astype(_BF16)
        b1, b2 = b_mlp1[l][None, :], b_mlp2[l][None, :]
        l1g, l1b, l2g, l2b = ln1_g[l][None, :], ln1_b[l][None, :], ln2_g[l][None, :], ln2_b[l][None, :]

        mod3 = _ada(cvecs, w_ada[l], b_ada[l]).reshape(8, 1, -1)

        def mixer_inputs(tokens, mod_row):
            h, cum_f, cum_b, mx = _lnmod(tokens, mod3, mod_row, w_lr, w_gf, w_gb, b_gf, b_gb, tri_lo, tri_up)
            n_tok = tokens.shape[1]
            p = _matmul(h.reshape(bsz * n_tok, d), w_p).reshape(bsz, n_tok, -1)
            return p, cum_f, cum_b, flags(mx)

        ctx_mod = lambda b: ctx_row
        p_c, cf_c, cb_c, (ff_c, fb_c) = mixer_inputs(ctx, ctx_mod)
        of_c, ob_c, s_f, s_b = _gla(p_c, cf_c, cb_c, ff_c, fb_c, zero_state, zero_state, tree_f, tree_b)
        if not last:
            ctx, h2_c = _mix(of_c, ob_c, p_c, ctx, mod3, ctx_mod, norm_g, pool_ctx, wpool, pscale, wout,
                             l1g, l1b, alpha)
            ctx = _mlp(h2_c.reshape(bsz * ctx_len, d), ctx.reshape(bsz * ctx_len, d), mod3, ctx_mod, ctx_len,
                       w1, b1, w2, b2, l2g, l2b, alpha).reshape(bsz, ctx_len, d)

        lat_mod = lambda b: b
        p_x, cf_x, cb_x, (ff_x, fb_x) = mixer_inputs(x, lat_mod)
        of_x, ob_x, _, _ = _gla(p_x, cf_x, cb_x, ff_x, fb_x, s_f, s_b, tree_f, tree_b)
        x, h2 = _mix(of_x, ob_x, p_x, x, mod3, lat_mod, norm_g, pool_lat, wpool, pscale, wout, l1g, l1b, alpha)
        x = _mlp(h2.reshape(bsz * t, d), x.reshape(bsz * t, d), mod3, lat_mod, t,
                 w1, b1, w2, b2, l2g, l2b, alpha).reshape(bsz, t, d)
    return x
```

```python
import functools
import math

import numpy as np
import jax
import jax.numpy as jnp
from jax import lax
from jax.experimental import pallas as pl
from jax.experimental.pallas import tpu as pltpu

GLA_HEADS = 4
GATE_RANK = 16
GATE_TAU = 16.0
POOL_WINDOWS = (2, 4, 8, 16)
GRID_W = 64
LN_EPS = 1e-6
RMS_EPS = 1e-6

LANES = 128
GLA_CHUNK = 128
POOL_TILE = 256
FAST_PATH_MAX_EXPONENT = 40.0
VMEM_LIMIT = 52 * 1024 * 1024

_F32 = jnp.float32
_BF16 = jnp.bfloat16
_NT = (((1,), (1,)), ((), ()))
_TN = (((0,), (0,)), ((), ()))


def _cparams(sem):
    return pltpu.CompilerParams(dimension_semantics=sem, vmem_limit_bytes=VMEM_LIMIT)


def _const_spec(shape, n_grid, single_buffer=False):
    idx = lambda *_: (0,) * len(shape)
    if single_buffer:
        return pl.BlockSpec(shape, idx, pipeline_mode=pl.Buffered(1))
    return pl.BlockSpec(shape, idx)


def _split2(a):
    hi = a.astype(_BF16)
    return hi, (a - hi.astype(_F32)).astype(_BF16)


def _dot(a, b):
    return jnp.dot(a, b, preferred_element_type=_F32)


def _cast_kernel(w_ref, o_ref):
    o_ref[...] = w_ref[...].astype(_BF16)


def _cast_bf16(w, rows):
    nl, r, c = w.shape
    return pl.pallas_call(
        _cast_kernel,
        out_shape=jax.ShapeDtypeStruct(w.shape, _BF16),
        grid=(nl, r // rows),
        in_specs=[pl.BlockSpec((None, rows, c), lambda l, i: (l, i, 0))],
        out_specs=pl.BlockSpec((None, rows, c), lambda l, i: (l, i, 0)),
        compiler_params=_cparams(("parallel", "parallel")),
        name="cast",
    )(w)


def _cast_win_kernel(w_ref, wp_ref, wlr_ref, *, gk, gv):
    o_v, o_lr, o_q = gk, gk + gv, gk + gv + 2 * GATE_RANK
    o_g = o_q + gk
    wp_ref[:, 0:gk] = w_ref[:, 0:gk].astype(_BF16)
    wp_ref[:, gk:2 * gk] = w_ref[:, o_q:o_g].astype(_BF16)
    wp_ref[:, 2 * gk:2 * gk + gv] = w_ref[:, o_v:o_lr].astype(_BF16)
    wp_ref[:, 2 * gk + gv:] = w_ref[:, o_g:].astype(_BF16)
    wlr_ref[...] = w_ref[:, o_lr:o_q].astype(_BF16)


def _cast_win(w_in, gk, gv):
    nl, d, cols = w_in.shape
    n_p = cols - 2 * GATE_RANK
    rows = 256
    return pl.pallas_call(
        functools.partial(_cast_win_kernel, gk=gk, gv=gv),
        out_shape=(jax.ShapeDtypeStruct((nl, d, n_p), _BF16),
                   jax.ShapeDtypeStruct((nl, d, 2 * GATE_RANK), _BF16)),
        grid=(nl, d // rows),
        in_specs=[pl.BlockSpec((None, rows, cols), lambda l, i: (l, i, 0))],
        out_specs=(pl.BlockSpec((None, rows, n_p), lambda l, i: (l, i, 0)),
                   pl.BlockSpec((None, rows, 2 * GATE_RANK), lambda l, i: (l, i, 0))),
        compiler_params=_cparams(("parallel", "parallel")),
        name="cast_win",
    )(w_in)


def _ada_kernel(c_ref, w_ref, b_ref, o_ref):
    c = c_ref[...]
    s = (c / (1.0 + jnp.exp(-c))).astype(_BF16)
    o_ref[...] = _dot(s, w_ref[...].astype(_BF16)) + b_ref[...]


def _ada(cvecs8, w_ada, b_ada):
    nl, d, n_out = w_ada.shape
    tn = 1024
    return pl.pallas_call(
        _ada_kernel,
        out_shape=jax.ShapeDtypeStruct((nl, 8, n_out), _F32),
        grid=(nl, n_out // tn),
        in_specs=[pl.BlockSpec((8, d), lambda l, j: (0, 0)),
                  pl.BlockSpec((None, d, tn), lambda l, j: (l, 0, j)),
                  pl.BlockSpec((None, 1, tn), lambda l, j: (l, 0, j))],
        out_specs=pl.BlockSpec((None, 8, tn), lambda l, j: (l, 0, j)),
        compiler_params=_cparams(("parallel", "parallel")),
        name="ada",
    )(cvecs8, w_ada, b_ada)


def _ln_plain(x):
    mu = jnp.mean(x, axis=-1, keepdims=True)
    xc = x - mu
    var = jnp.mean(xc * xc, axis=-1, keepdims=True)
    return xc * lax.rsqrt(var + LN_EPS)


def _log_sigmoid(z):
    return jnp.minimum(z, 0.0) - jnp.log1p(jnp.exp(-jnp.abs(z)))


def _inproj_kernel(x_ref, sh_ref, sc_ref, wp_ref, wlr_ref, wg_ref, bg_ref, tri_ref,
                   p_ref, cf_ref, cb_ref, mx_ref, *, chunk, n_col):
    h = (_ln_plain(x_ref[...]) * (1.0 + sc_ref[...]) + sh_ref[...]).astype(_BF16)
    tr = h.shape[0]
    tn = p_ref.shape[1] // n_col
    for j in range(n_col):
        p_ref[:, j * tn:(j + 1) * tn] = _dot(h, wp_ref[:, j * tn:(j + 1) * tn])
    lr_hi, lr_lo = _split2(_dot(h, wlr_ref[...]))
    wg_hi, wg_lo = _split2(wg_ref[...])
    z = _dot(lr_hi, wg_hi) + _dot(lr_hi, wg_lo) + _dot(lr_lo, wg_hi) + bg_ref[...]
    la = _log_sigmoid(z) * (1.0 / GATE_TAU)
    gk = la.shape[1] // 2
    half = chunk // 2
    pair = tri_ref.shape[-1]
    for s in range(tr // pair):
        rows = slice(s * pair, (s + 1) * pair)
        a1 = la[rows].astype(_BF16)
        r1 = la[rows] - a1.astype(_F32)
        a2 = r1.astype(_BF16)
        a3 = (r1 - a2.astype(_F32)).astype(_BF16)
        cum_f = _dot(tri_ref[0], a1[:, :gk]) + _dot(tri_ref[0], a2[:, :gk]) + _dot(tri_ref[0], a3[:, :gk])
        cum_b = _dot(tri_ref[1], a1[:, gk:]) + _dot(tri_ref[1], a2[:, gk:]) + _dot(tri_ref[1], a3[:, gk:])
        cf_ref[rows, :] = cum_f
        cb_ref[rows, :] = cum_b
        for c in range(pair // chunk):
            lo = c * chunk
            ci = s * (pair // chunk) + c
            ef = cum_f[lo:lo + chunk] - cum_f[lo + half - 1:lo + half]
            eb = cum_b[lo:lo + chunk] - cum_b[lo + half:lo + half + 1]
            mx_ref[ci, 0:1, :] = jnp.max(jnp.abs(ef), axis=0, keepdims=True)
            mx_ref[ci, 1:2, :] = jnp.max(jnp.abs(eb), axis=0, keepdims=True)


def _inproj(x, mod, layer, mod_row, w_p, w_lr, w_g, b_g, tri, chunk):
    bsz, t, d = x.shape
    n_p = w_p.shape[-1]
    gk = w_g.shape[1] // 2
    tr = min(t, 512)
    c3 = lambda shape: pl.BlockSpec((None,) + shape, lambda b, i: (layer,) + (0,) * len(shape),
                                    pipeline_mode=pl.Buffered(1))
    return pl.pallas_call(
        functools.partial(_inproj_kernel, chunk=chunk, n_col=4),
        out_shape=(jax.ShapeDtypeStruct((bsz, t, n_p), _F32),
                   jax.ShapeDtypeStruct((bsz, t, gk), _F32),
                   jax.ShapeDtypeStruct((bsz, t, gk), _F32),
                   jax.ShapeDtypeStruct((bsz, t // chunk, 2, gk), _F32)),
        grid=(bsz, t // tr),
        in_specs=[pl.BlockSpec((None, tr, d), lambda b, i: (b, i, 0)),
                  pl.BlockSpec((None, None, 1, d), lambda b, i: (layer, mod_row(b), 0, 0)),
                  pl.BlockSpec((None, None, 1, d), lambda b, i: (layer, mod_row(b), 0, 1)),
                  c3(w_p.shape[1:]), c3(w_lr.shape[1:]),
                  _const_spec(w_g.shape, 2), _const_spec(b_g.shape, 2), _const_spec(tri.shape, 2)],
        out_specs=(pl.BlockSpec((None, tr, n_p), lambda b, i: (b, i, 0)),
                   pl.BlockSpec((None, tr, gk), lambda b, i: (b, i, 0)),
                   pl.BlockSpec((None, tr, gk), lambda b, i: (b, i, 0)),
                   pl.BlockSpec((None, tr // chunk, 2, gk), lambda b, i: (b, i, 0, 0))),
        compiler_params=_cparams(("parallel", "parallel")),
        name="inproj",
    )(x, mod, mod, w_p, w_lr, w_g, b_g, tri)


def _chunk_fast(q, k, v, cum, st, *, fwd, scale):
    c = q.shape[0]
    if fwd:
        ref, tot = cum[c // 2 - 1:c // 2, :], cum[c - 1:c, :]
    else:
        ref, tot = cum[c // 2:c // 2 + 1, :], cum[0:1, :]
    e = cum - ref
    qt = q * jnp.exp(e) * scale
    kt = k * jnp.exp(-e)
    a = lax.dot_general(qt.astype(_BF16), kt.astype(_BF16), _NT, preferred_element_type=_F32)
    row = lax.broadcasted_iota(jnp.int32, (c, c), 0)
    col = lax.broadcasted_iota(jnp.int32, (c, c), 1)
    a = jnp.where((col <= row) if fwd else (col >= row), a, 0.0)
    vb = v.astype(_BF16)
    o = _dot(a.astype(_BF16), vb)
    qh = (qt * jnp.exp(ref)).astype(_BF16)
    o = o + lax.dot_general(qh, st.astype(_BF16), _NT, preferred_element_type=_F32)
    kh = (kt * jnp.exp(tot - ref)).astype(_BF16)
    st_new = st * jnp.exp(tot) + lax.dot_general(vb, kh, _TN, preferred_element_type=_F32)
    return o, st_new


def _chunk_robust(q, k, v, cum, st, *, p_ref, fwd, scale):
    c = q.shape[0]
    row = lax.broadcasted_iota(jnp.int32, (c, c), 0)
    col = lax.broadcasted_iota(jnp.int32, (c, c), 1)
    rowv = lax.broadcasted_iota(jnp.int32, q.shape, 0)
    qs = q * scale

    diag = lax.dot_general(qs.astype(_BF16), k.astype(_BF16), _NT, preferred_element_type=_F32)
    a = jnp.where(row == col, diag, 0.0)
    for l in range(int(math.log2(c))):
        e = jnp.dot(p_ref[l], cum, precision=lax.Precision.HIGHEST, preferred_element_type=_F32)
        ex = jnp.exp(jnp.minimum(e, 0.0))
        later = ((rowv >> l) & 1) == (1 if fwd else 0)
        qt = jnp.where(later, qs * ex, 0.0).astype(_BF16)
        kt = jnp.where(later, 0.0, k * ex).astype(_BF16)
        al = lax.dot_general(qt, kt, _NT, preferred_element_type=_F32)
        a = a + jnp.where((row >> (l + 1)) == (col >> (l + 1)), al, 0.0)
    vb = v.astype(_BF16)
    o = _dot(a.astype(_BF16), vb)
    tot = cum[c - 1:c, :] if fwd else cum[0:1, :]
    qh = (qs * jnp.exp(cum)).astype(_BF16)
    o = o + lax.dot_general(qh, st.astype(_BF16), _NT, preferred_element_type=_F32)
    kh = (k * jnp.exp(tot - cum)).astype(_BF16)
    st_new = st * jnp.exp(tot) + lax.dot_general(vb, kh, _TN, preferred_element_type=_F32)
    return o, st_new


def _gla_kernel(flag_ref, k_ref, q_ref, v_ref, cf_ref, cb_ref, s0f_ref, s0b_ref, pf_ref, pb_ref, ng_ref,
                o_ref, sf_ref, sb_ref, of_scr, ob_scr, *, chunk, scale, unroll):
    t = k_ref.shape[0]
    nc = t // chunk
    flag = flag_ref[pl.program_id(0) * pl.num_programs(1) + pl.program_id(1)]

    def scan(chunk_f, chunk_b, unroll):
        def step(i, carry):
            st_f, st_b = carry
            rf = pl.ds(pl.multiple_of(i * chunk, chunk), chunk)
            rb = pl.ds(pl.multiple_of((nc - 1 - i) * chunk, chunk), chunk)
            o_f, st_f = chunk_f(q_ref[rf, :], k_ref[rf, :], v_ref[rf, :], cf_ref[rf, :], st_f)
            o_b, st_b = chunk_b(q_ref[rb, :], k_ref[rb, :], v_ref[rb, :], cb_ref[rb, :], st_b)
            of_scr[rf, :] = o_f
            ob_scr[rb, :] = o_b
            return st_f, st_b

        st_f, st_b = lax.fori_loop(0, nc, step, (s0f_ref[...], s0b_ref[...]), unroll=unroll)
        sf_ref[...] = st_f
        sb_ref[...] = st_b

    @pl.when(flag == 0)
    def _():
        scan(functools.partial(_chunk_fast, fwd=True, scale=scale),
             functools.partial(_chunk_fast, fwd=False, scale=scale), unroll)

    @pl.when(flag != 0)
    def _():
        scan(functools.partial(_chunk_robust, p_ref=pf_ref, fwd=True, scale=scale),
             functools.partial(_chunk_robust, p_ref=pb_ref, fwd=False, scale=scale), 1)

    def finish(i, _):
        rows = pl.ds(pl.multiple_of(i * chunk, chunk), chunk)
        o = of_scr[rows, :] + ob_scr[rows, :]
        o = o * lax.rsqrt(jnp.mean(o * o, axis=-1, keepdims=True) + RMS_EPS)
        o_ref[rows, :] = (o * ng_ref[...]).astype(o_ref.dtype)
        return 0

    lax.fori_loop(0, nc, finish, 0, unroll=min(nc, 8))


def _gla(p, cum_f, cum_b, flags, s0_f, s0_b, tree_f, tree_b, norm_g):
    bsz, t, _ = p.shape
    gk = cum_f.shape[-1]
    nh = GLA_HEADS
    dk = gk // nh
    dv = norm_g.shape[-1]
    c = tree_f.shape[-1]
    seq = lambda width, col: pl.BlockSpec((None, t, width), lambda b, h, *_: (b, 0, col(h)))
    st_spec = pl.BlockSpec((None, None, dv, dk), lambda b, h, *_: (b, h, 0, 0))
    cst = lambda shape: pl.BlockSpec(shape, lambda b, h, *_: (0,) * len(shape))
    grid_spec = pltpu.PrefetchScalarGridSpec(
        num_scalar_prefetch=1,
        grid=(bsz, nh),
        in_specs=[seq(dk, lambda h: h), seq(dk, lambda h: nh + h), seq(dv, lambda h: (2 * gk) // dv + h),
                  seq(dk, lambda h: h), seq(dk, lambda h: h),
                  st_spec, st_spec, cst(tree_f.shape), cst(tree_b.shape), cst(norm_g.shape)],
        out_specs=(seq(dv, lambda h: h), st_spec, st_spec),
        scratch_shapes=[pltpu.VMEM((t, dv), _F32), pltpu.VMEM((t, dv), _F32)],
    )
    return pl.pallas_call(
        functools.partial(_gla_kernel, chunk=c, scale=dk ** -0.5, unroll=min(t // c, 4)),
        out_shape=(jax.ShapeDtypeStruct((bsz, t, nh * dv), _BF16),
                   jax.ShapeDtypeStruct((bsz, nh, dv, dk), _F32), jax.ShapeDtypeStruct((bsz, nh, dv, dk), _F32)),
        grid_spec=grid_spec,
        compiler_params=_cparams(("parallel", "parallel")),
        name="gla",
    )(flags, p, p, p, cum_f, cum_b, s0_f, s0_b, tree_f, tree_b, norm_g)


def _mix_kernel(o_ref, g_ref, u_ref, x_ref, g1_ref, sh2_ref, sc2_ref,
                band_ref, invc_ref, wp_ref, ps_ref, wo_ref, lg_ref, lb_ref,
                xo_ref, h2_ref, *, alpha, pool_dim):
    tm = x_ref.shape[0]
    g = g_ref[...]
    gla = (o_ref[...].astype(_F32) * (g / (1.0 + jnp.exp(-g)))).astype(_BF16)
    pooled = []
    for s in range(tm // POOL_TILE):
        cols = []
        for gi in range(u_ref.shape[1] // pool_dim):
            ug = u_ref[s * POOL_TILE:(s + 1) * POOL_TILE, gi * pool_dim:(gi + 1) * pool_dim]
            u_hi, u_lo = _split2(ug)
            mean = (_dot(band_ref[gi], u_hi) + _dot(band_ref[gi], u_lo)) * invc_ref[gi]
            cols.append(_dot((mean - ug).astype(_BF16), wp_ref[gi]))
        pooled.append(jnp.concatenate(cols, axis=-1))
    pool = (jnp.concatenate(pooled, axis=0) * ps_ref[...]).astype(_BF16)
    v_w = gla.shape[1]
    mixed = _dot(gla, wo_ref[0:v_w, :]) + _dot(pool, wo_ref[v_w:, :])
    xn = _ln_plain(alpha * x_ref[...] + g1_ref[...] * mixed) * lg_ref[...] + lb_ref[...]
    xo_ref[...] = xn
    h2_ref[...] = (_ln_plain(xn) * (1.0 + sc2_ref[...]) + sh2_ref[...]).astype(_BF16)


def _mix(o_n, p, x, mod, layer, mod_row, band, invc, w_pool, pool_scale, w_out, ln_g, ln_b, alpha):
    bsz, t, d = x.shape
    v = o_n.shape[-1]
    tm = POOL_TILE
    row = lambda col: pl.BlockSpec((None, None, 1, d), lambda b, i: (layer, mod_row(b), 0, col))
    tok = lambda width, col: pl.BlockSpec((None, tm, width), lambda b, i: (b, i, col))
    lyr = lambda shape: pl.BlockSpec((None,) + shape, lambda b, i: (layer,) + (0,) * len(shape))
    cst = lambda a: _const_spec(a.shape, 2)
    return pl.pallas_call(
        functools.partial(_mix_kernel, alpha=alpha, pool_dim=w_pool.shape[-1]),
        out_shape=(jax.ShapeDtypeStruct((bsz, t, d), _F32), jax.ShapeDtypeStruct((bsz, t, d), _BF16)),
        grid=(bsz, t // tm),
        in_specs=[tok(v, 0), tok(v, 2), tok(v, 3), tok(d, 0), row(2), row(3), row(4),
                  cst(band), cst(invc), lyr(w_pool.shape[1:]), cst(pool_scale), lyr(w_out.shape[1:]),
                  cst(ln_g), cst(ln_b)],
        out_specs=(tok(d, 0), tok(d, 0)),
        compiler_params=_cparams(("parallel", "parallel")),
        name="mix",
    )(o_n, p, p, x, mod, mod, mod, band, invc, w_pool, pool_scale, w_out, ln_g, ln_b)


def _mlp_kernel(h_ref, w1_ref, b1_ref, w2_ref, b2_ref, x_ref, g2_ref, lg_ref, lb_ref, o_ref, acc_ref, *, alpha):
    f = pl.program_id(1)

    @pl.when(f == 0)
    def _():
        acc_ref[...] = jnp.zeros_like(acc_ref)

    a = jnp.maximum(_dot(h_ref[...], w1_ref[...]) + b1_ref[...], 0.0)
    acc_ref[...] += _dot((a * a).astype(_BF16), w2_ref[...])

    @pl.when(f == pl.num_programs(1) - 1)
    def _():
        y = acc_ref[...] + b2_ref[...]
        o_ref[...] = _ln_plain(alpha * x_ref[...] + g2_ref[...] * y) * lg_ref[...] + lb_ref[...]


def _mlp(h2, x, mod, layer, mod_row, rows_per_batch, w1, b1, w2, b2, ln_g, ln_b, alpha):
    n, d = x.shape
    dff = w1.shape[-1]
    tm, tf = min(512, rows_per_batch), 1024
    tiles_per_batch = rows_per_batch // tm
    cst = lambda a: _const_spec(a.shape, 2)
    return pl.pallas_call(
        functools.partial(_mlp_kernel, alpha=alpha),
        out_shape=jax.ShapeDtypeStruct((n, d), _F32),
        grid=(n // tm, dff // tf),
        in_specs=[pl.BlockSpec((tm, d), lambda i, f: (i, 0)),
                  pl.BlockSpec((None, d, tf), lambda i, f: (layer, 0, f)),
                  pl.BlockSpec((1, tf), lambda i, f: (0, f)),
                  pl.BlockSpec((None, tf, d), lambda i, f: (layer, f, 0)),
                  cst(b2),
                  pl.BlockSpec((tm, d), lambda i, f: (i, 0)),
                  pl.BlockSpec((None, None, 1, d), lambda i, f: (layer, mod_row(i // tiles_per_batch), 0, 5)),
                  cst(ln_g), cst(ln_b)],
        out_specs=pl.BlockSpec((tm, d), lambda i, f: (i, 0)),
        scratch_shapes=[pltpu.VMEM((tm, d), _F32)],
        compiler_params=_cparams(("parallel", "arbitrary")),
        name="mlp",
    )(h2, w1, b1, w2, b2, x, mod, ln_g, ln_b)


def _pool_operators(seg_len):
    pos = np.arange(POOL_TILE)
    seg_start = (pos // seg_len) * seg_len
    seg_end = seg_start + seg_len - 1
    band = np.zeros((len(POOL_WINDOWS), POOL_TILE, POOL_TILE), np.float32)
    invc = np.zeros((len(POOL_WINDOWS), POOL_TILE, 2 * LANES), np.float32)
    for gi, w in enumerate(POOL_WINDOWS):
        lo = np.maximum(pos - w // 2, seg_start)
        hi = np.minimum(pos + w // 2 - 1, seg_end)
        for r in range(POOL_TILE):
            band[gi, r, lo[r]:hi[r] + 1] = 1.0
            invc[gi, r, :] = 1.0 / (hi[r] - lo[r] + 1)
    return jnp.asarray(band, _BF16), jnp.asarray(invc)


def _cumsum_operators(chunk, span):
    r = np.arange(span)
    same = (r[:, None] // chunk) == (r[None, :] // chunk)
    lo = same & (r[None, :] <= r[:, None])
    up = same & (r[None, :] >= r[:, None])
    return jnp.asarray(np.stack([lo, up]).astype(np.float32), _BF16)


def _tree_matrices(c, fwd):
    levels = int(math.log2(c))
    mats = np.zeros((levels, c, c), np.float32)
    for l in range(levels):
        m = 1 << l
        for r in range(c):
            mid = (r // (2 * m)) * 2 * m + m
            ref = mid - 1 if fwd else mid
            sign = 1.0 if (r >= mid) == fwd else -1.0
            mats[l, r, r] += sign
            mats[l, r, ref] -= sign
    return jnp.asarray(mats)


def kernel(x, c, ctx, c_ctx, w_ada, b_ada, w_in, w_gate_up, b_gate, gla_norm_g, w_pool, pool_scale, w_out,
           ln1_g, ln1_b, w_mlp1, b_mlp1, w_mlp2, b_mlp2, ln2_g, ln2_b):
    bsz, t, d = x.shape
    ctx_len = ctx.shape[1]
    depth = w_in.shape[0]
    gk = w_gate_up.shape[-1]
    dv = gla_norm_g.shape[-1]
    gv = dv * GLA_HEADS
    dk = gk // GLA_HEADS
    alpha = (2 * depth) ** 0.25
    chunk = GLA_CHUNK
    ctx_row = bsz

    cvecs8 = jnp.concatenate([c, c_ctx[None, :], jnp.zeros((8 - bsz - 1, d), _F32)], axis=0)
    mod = _ada(cvecs8, w_ada, b_ada[:, None, :]).reshape(depth, 8, 1, -1)

    w_p, w_lr = _cast_win(w_in, gk, gv)
    w_out_b = _cast_bf16(w_out, 512)
    w_pool_b = _cast_bf16(w_pool.reshape(depth, -1, w_pool.shape[-1]), 512).reshape(w_pool.shape)
    w1_b = _cast_bf16(w_mlp1, 128)
    w2_b = _cast_bf16(w_mlp2, 512)

    tri = _cumsum_operators(chunk, 2 * chunk)
    tree_f, tree_b = _tree_matrices(chunk, True), _tree_matrices(chunk, False)
    pool_lat, pool_ctx = _pool_operators(GRID_W), _pool_operators(ctx_len)
    zero_state = jnp.zeros((bsz, GLA_HEADS, dv, dk), _F32)
    zpad = jnp.zeros((GATE_RANK, gk), _F32)

    def flags(mx):
        m = mx.reshape(bsz, -1, 2, GLA_HEADS, dk).max(axis=(1, 2, 4))
        return (m > FAST_PATH_MAX_EXPONENT).astype(jnp.int32).reshape(-1)

    for l in range(depth):
        last = l == depth - 1
        w_g = jnp.concatenate([jnp.concatenate([w_gate_up[l, 0], zpad], axis=1),
                               jnp.concatenate([zpad, w_gate_up[l, 1]], axis=1)], axis=0)
        b_g = jnp.concatenate([b_gate[l, 0], b_gate[l, 1]])[None, :]
        norm_g = gla_norm_g[l][None, :]
        pscale = pool_scale[l][None, :]
        b1, b2 = b_mlp1[l][None, :], b_mlp2[l][None, :]
        l1g, l1b, l2g, l2b = ln1_g[l][None, :], ln1_b[l][None, :], ln2_g[l][None, :], ln2_b[l][None, :]

        def mixer(tokens, mod_row, s0_f, s0_b):
            p, cum_f, cum_b, mx = _inproj(tokens, mod, l, mod_row, w_p, w_lr, w_g, b_g, tri, chunk)
            o_n, s_f, s_b = _gla(p, cum_f, cum_b, flags(mx), s0_f, s0_b, tree_f, tree_b, norm_g)
            return p, o_n, s_f, s_b

        def rest(tokens, p, o_n, mod_row, pool_ops):
            n_tok = tokens.shape[1]
            xm, h2 = _mix(o_n, p, tokens, mod, l, mod_row, pool_ops[0], pool_ops[1], w_pool_b, pscale, w_out_b,
                          l1g, l1b, alpha)
            return _mlp(h2.reshape(bsz * n_tok, d), xm.reshape(bsz * n_tok, d), mod, l, mod_row, n_tok,
                        w1_b, b1, w2_b, b2, l2g, l2b, alpha).reshape(bsz, n_tok, d)

        ctx_mod = lambda b: ctx_row
        p_c, on_c, s_f, s_b = mixer(ctx, ctx_mod, zero_state, zero_state)
        if not last:
            ctx = rest(ctx, p_c, on_c, ctx_mod, pool_ctx)

        lat_mod = lambda b: b
        p_x, on_x, _, _ = mixer(x, lat_mod, s_f, s_b)
        x = rest(x, p_x, on_x, lat_mod, pool_lat)
    return x
```

```python
import functools
import math

import numpy as np
import jax
import jax.numpy as jnp
from jax import lax
from jax.experimental import pallas as pl
from jax.experimental.pallas import tpu as pltpu

GLA_HEADS = 4
GATE_RANK = 16
GATE_TAU = 16.0
POOL_WINDOWS = (2, 4, 8, 16)
GRID_W = 64
LN_EPS = 1e-6
RMS_EPS = 1e-6

LANES = 128
GLA_CHUNK = 128
POOL_TILE = 256
FAST_PATH_MAX_EXPONENT = 40.0
VMEM_LIMIT = 52 * 1024 * 1024

_F32 = jnp.float32
_BF16 = jnp.bfloat16
_NT = (((1,), (1,)), ((), ()))
_TN = (((0,), (0,)), ((), ()))


def _cparams(sem):
    return pltpu.CompilerParams(dimension_semantics=sem, vmem_limit_bytes=VMEM_LIMIT)


def _const_spec(shape, n_grid, single_buffer=False):
    idx = lambda *_: (0,) * len(shape)
    if single_buffer:
        return pl.BlockSpec(shape, idx, pipeline_mode=pl.Buffered(1))
    return pl.BlockSpec(shape, idx)


def _split2(a):
    hi = a.astype(_BF16)
    return hi, (a - hi.astype(_F32)).astype(_BF16)


def _dot(a, b):
    return jnp.dot(a, b, preferred_element_type=_F32)


def _cast_kernel(w_ref, o_ref):
    o_ref[...] = w_ref[...].astype(_BF16)


def _cast_bf16(w, rows):
    nl, r, c = w.shape
    return pl.pallas_call(
        _cast_kernel,
        out_shape=jax.ShapeDtypeStruct(w.shape, _BF16),
        grid=(nl, r // rows),
        in_specs=[pl.BlockSpec((None, rows, c), lambda l, i: (l, i, 0))],
        out_specs=pl.BlockSpec((None, rows, c), lambda l, i: (l, i, 0)),
        compiler_params=_cparams(("parallel", "parallel")),
        name="cast",
    )(w)


def _cast_win_kernel(wt_ref, wp_ref, wlr_ref, *, gk, gv):
    o_v, o_lr, o_q = gk, gk + gv, gk + gv + 2 * GATE_RANK
    o_g = o_q + gk
    pieces = [(0, gk), (o_q, gk), (o_v, gv), (o_g, wt_ref.shape[0] - o_g)]
    dst = 0
    for src, width in pieces:
        for off in range(0, width, 512):
            w = min(512, width - off)
            wp_ref[:, dst + off:dst + off + w] = wt_ref[src + off:src + off + w, :].T.astype(_BF16)
        dst += width
    wlr_ref[...] = wt_ref[o_lr:o_q, :].T.astype(_BF16)


def _cast_win(w_in_t, gk, gv):
    nl, cols, d = w_in_t.shape
    n_p = cols - 2 * GATE_RANK
    rows = 256
    return pl.pallas_call(
        functools.partial(_cast_win_kernel, gk=gk, gv=gv),
        out_shape=(jax.ShapeDtypeStruct((nl, d, n_p), _BF16),
                   jax.ShapeDtypeStruct((nl, d, 2 * GATE_RANK), _BF16)),
        grid=(nl, d // rows),
        in_specs=[pl.BlockSpec((None, cols, rows), lambda l, i: (l, 0, i))],
        out_specs=(pl.BlockSpec((None, rows, n_p), lambda l, i: (l, i, 0)),
                   pl.BlockSpec((None, rows, 2 * GATE_RANK), lambda l, i: (l, i, 0))),
        compiler_params=_cparams(("parallel", "parallel")),
        name="cast_win",
    )(w_in_t)


def _ada_kernel(c_ref, w_ref, b_ref, o_ref):
    c = c_ref[...]
    s = (c / (1.0 + jnp.exp(-c))).astype(_BF16)
    o_ref[...] = _dot(s, w_ref[...].astype(_BF16)) + b_ref[...]


def _ada(cvecs8, w_ada, b_ada):
    nl, d, n_out = w_ada.shape
    tn = 1024
    return pl.pallas_call(
        _ada_kernel,
        out_shape=jax.ShapeDtypeStruct((nl, 8, n_out), _F32),
        grid=(nl, n_out // tn),
        in_specs=[pl.BlockSpec((8, d), lambda l, j: (0, 0)),
                  pl.BlockSpec((None, d, tn), lambda l, j: (l, 0, j)),
                  pl.BlockSpec((None, 1, tn), lambda l, j: (l, 0, j))],
        out_specs=pl.BlockSpec((None, 8, tn), lambda l, j: (l, 0, j)),
        compiler_params=_cparams(("parallel", "parallel")),
        name="ada",
    )(cvecs8, w_ada, b_ada)


def _ln_plain(x):
    mu = jnp.mean(x, axis=-1, keepdims=True)
    xc = x - mu
    var = jnp.mean(xc * xc, axis=-1, keepdims=True)
    return xc * lax.rsqrt(var + LN_EPS)


def _log_sigmoid(z):
    return jnp.minimum(z, 0.0) - jnp.log1p(jnp.exp(-jnp.abs(z)))


def _inproj_kernel(x_ref, sh_ref, sc_ref, wp_ref, wlr_ref, wg_ref, bg_ref, tri_ref,
                   p_ref, cf_ref, cb_ref, mx_ref, *, chunk, n_col):
    h = (_ln_plain(x_ref[...]) * (1.0 + sc_ref[...]) + sh_ref[...]).astype(_BF16)
    tr = h.shape[0]
    tn = p_ref.shape[1] // n_col

    def proj(j):
        p_ref[:, j * tn:(j + 1) * tn] = _dot(h, wp_ref[:, j * tn:(j + 1) * tn])

    lr = _dot(h, wlr_ref[...])
    proj(0)
    lr_hi, lr_lo = _split2(lr)
    z = _dot(jnp.concatenate([lr_hi, lr_lo, lr_hi], axis=1), wg_ref[...]) + bg_ref[...]
    proj(1)
    la = _log_sigmoid(z) * (1.0 / GATE_TAU)
    gk = la.shape[1] // 2
    half = chunk // 2
    pair = tri_ref.shape[-1]
    for s in range(tr // pair):
        rows = slice(s * pair, (s + 1) * pair)
        a_hi, a_lo = _split2(la[rows])
        cum_f = _dot(tri_ref[0], a_hi[:, :gk]) + _dot(tri_ref[0], a_lo[:, :gk])
        cum_b = _dot(tri_ref[1], a_hi[:, gk:]) + _dot(tri_ref[1], a_lo[:, gk:])
        cf_ref[rows, :] = cum_f
        cb_ref[rows, :] = cum_b
        for c in range(pair // chunk):
            lo = c * chunk
            ci = s * (pair // chunk) + c
            ef = cum_f[lo:lo + chunk] - cum_f[lo + half - 1:lo + half]
            eb = cum_b[lo:lo + chunk] - cum_b[lo + half:lo + half + 1]
            mx_ref[ci, 0:1, :] = jnp.max(jnp.abs(ef), axis=0, keepdims=True)
            mx_ref[ci, 1:2, :] = jnp.max(jnp.abs(eb), axis=0, keepdims=True)
    for j in range(2, n_col):
        proj(j)


def _inproj(x, mod, layer, mod_row, w_p, w_lr, w_g, b_g, tri, chunk):
    bsz, t, d = x.shape
    n_p = w_p.shape[-1]
    gk = w_g.shape[1] // 2
    tr = min(t, 512)
    c3 = lambda shape: pl.BlockSpec((None,) + shape, lambda b, i: (layer,) + (0,) * len(shape),
                                    pipeline_mode=pl.Buffered(1))
    return pl.pallas_call(
        functools.partial(_inproj_kernel, chunk=chunk, n_col=4),
        out_shape=(jax.ShapeDtypeStruct((bsz, t, n_p), _F32),
                   jax.ShapeDtypeStruct((bsz, t, gk), _F32),
                   jax.ShapeDtypeStruct((bsz, t, gk), _F32),
                   jax.ShapeDtypeStruct((bsz, t // chunk, 2, gk), _F32)),
        grid=(bsz, t // tr),
        in_specs=[pl.BlockSpec((None, tr, d), lambda b, i: (b, i, 0)),
                  pl.BlockSpec((None, None, 1, d), lambda b, i: (layer, mod_row(b), 0, 0)),
                  pl.BlockSpec((None, None, 1, d), lambda b, i: (layer, mod_row(b), 0, 1)),
                  c3(w_p.shape[1:]), c3(w_lr.shape[1:]),
                  _const_spec(w_g.shape, 2), _const_spec(b_g.shape, 2), _const_spec(tri.shape, 2)],
        out_specs=(pl.BlockSpec((None, tr, n_p), lambda b, i: (b, i, 0)),
                   pl.BlockSpec((None, tr, gk), lambda b, i: (b, i, 0)),
                   pl.BlockSpec((None, tr, gk), lambda b, i: (b, i, 0)),
                   pl.BlockSpec((None, tr // chunk, 2, gk), lambda b, i: (b, i, 0, 0))),
        compiler_params=_cparams(("parallel", "parallel")),
        name="inproj",
    )(x, mod, mod, w_p, w_lr, w_g, b_g, tri)


def _chunk_fast(q, k, v, cum, st, *, fwd, scale):
    c = q.shape[0]
    if fwd:
        ref, tot = cum[c // 2 - 1:c // 2, :], cum[c - 1:c, :]
    else:
        ref, tot = cum[c // 2:c // 2 + 1, :], cum[0:1, :]
    e = cum - ref
    qt = q * jnp.exp(e) * scale
    kt = k * jnp.exp(-e)
    a = lax.dot_general(qt.astype(_BF16), kt.astype(_BF16), _NT, preferred_element_type=_F32)
    row = lax.broadcasted_iota(jnp.int32, (c, c), 0)
    col = lax.broadcasted_iota(jnp.int32, (c, c), 1)
    a = jnp.where((col <= row) if fwd else (col >= row), a, 0.0)
    vb = v.astype(_BF16)
    o = _dot(a.astype(_BF16), vb)
    qh = (qt * jnp.exp(ref)).astype(_BF16)
    o = o + lax.dot_general(qh, st.astype(_BF16), _NT, preferred_element_type=_F32)
    kh = (kt * jnp.exp(tot - ref)).astype(_BF16)
    st_new = st * jnp.exp(tot) + lax.dot_general(vb, kh, _TN, preferred_element_type=_F32)
    return o, st_new


def _chunk_robust(q, k, v, cum, st, *, p_ref, fwd, scale):
    c = q.shape[0]
    row = lax.broadcasted_iota(jnp.int32, (c, c), 0)
    col = lax.broadcasted_iota(jnp.int32, (c, c), 1)
    rowv = lax.broadcasted_iota(jnp.int32, q.shape, 0)
    qs = q * scale

    diag = lax.dot_general(qs.astype(_BF16), k.astype(_BF16), _NT, preferred_element_type=_F32)
    a = jnp.where(row == col, diag, 0.0)
    for l in range(int(math.log2(c))):
        e = jnp.dot(p_ref[l], cum, precision=lax.Precision.HIGHEST, preferred_element_type=_F32)
        ex = jnp.exp(jnp.minimum(e, 0.0))
        later = ((rowv >> l) & 1) == (1 if fwd else 0)
        qt = jnp.where(later, qs * ex, 0.0).astype(_BF16)
        kt = jnp.where(later, 0.0, k * ex).astype(_BF16)
        al = lax.dot_general(qt, kt, _NT, preferred_element_type=_F32)
        a = a + jnp.where((row >> (l + 1)) == (col >> (l + 1)), al, 0.0)
    vb = v.astype(_BF16)
    o = _dot(a.astype(_BF16), vb)
    tot = cum[c - 1:c, :] if fwd else cum[0:1, :]
    qh = (qs * jnp.exp(cum)).astype(_BF16)
    o = o + lax.dot_general(qh, st.astype(_BF16), _NT, preferred_element_type=_F32)
    kh = (k * jnp.exp(tot - cum)).astype(_BF16)
    st_new = st * jnp.exp(tot) + lax.dot_general(vb, kh, _TN, preferred_element_type=_F32)
    return o, st_new


def _gla_kernel(flag_ref, k_ref, q_ref, v_ref, cf_ref, cb_ref, s0f_ref, s0b_ref, pf_ref, pb_ref, ng_ref,
                o_ref, sf_ref, sb_ref, of_scr, ob_scr, *, chunk, scale, unroll):
    t = k_ref.shape[0]
    nc = t // chunk
    flag = flag_ref[pl.program_id(0) * pl.num_programs(1) + pl.program_id(1)]

    def scan(chunk_f, chunk_b, unroll):
        def step(i, carry):
            st_f, st_b = carry
            rf = pl.ds(pl.multiple_of(i * chunk, chunk), chunk)
            rb = pl.ds(pl.multiple_of((nc - 1 - i) * chunk, chunk), chunk)
            o_f, st_f = chunk_f(q_ref[rf, :], k_ref[rf, :], v_ref[rf, :], cf_ref[rf, :], st_f)
            o_b, st_b = chunk_b(q_ref[rb, :], k_ref[rb, :], v_ref[rb, :], cb_ref[rb, :], st_b)
            of_scr[rf, :] = o_f
            ob_scr[rb, :] = o_b
            return st_f, st_b

        st_f, st_b = lax.fori_loop(0, nc, step, (s0f_ref[...], s0b_ref[...]), unroll=unroll)
        sf_ref[...] = st_f
        sb_ref[...] = st_b

    @pl.when(flag == 0)
    def _():
        scan(functools.partial(_chunk_fast, fwd=True, scale=scale),
             functools.partial(_chunk_fast, fwd=False, scale=scale), unroll)

    @pl.when(flag != 0)
    def _():
        scan(functools.partial(_chunk_robust, p_ref=pf_ref, fwd=True, scale=scale),
             functools.partial(_chunk_robust, p_ref=pb_ref, fwd=False, scale=scale), 1)

    def finish(i, _):
        rows = pl.ds(pl.multiple_of(i * chunk, chunk), chunk)
        o = of_scr[rows, :] + ob_scr[rows, :]
        o = o * lax.rsqrt(jnp.mean(o * o, axis=-1, keepdims=True) + RMS_EPS)
        o_ref[rows, :] = (o * ng_ref[...]).astype(o_ref.dtype)
        return 0

    lax.fori_loop(0, nc, finish, 0, unroll=min(nc, 8))


def _gla(p, cum_f, cum_b, flags, s0_f, s0_b, tree_f, tree_b, norm_g):
    bsz, t, _ = p.shape
    gk = cum_f.shape[-1]
    nh = GLA_HEADS
    dk = gk // nh
    dv = norm_g.shape[-1]
    c = tree_f.shape[-1]
    seq = lambda width, col: pl.BlockSpec((None, t, width), lambda b, h, *_: (b, 0, col(h)))
    st_spec = pl.BlockSpec((None, None, dv, dk), lambda b, h, *_: (b, h, 0, 0))
    cst = lambda shape: pl.BlockSpec(shape, lambda b, h, *_: (0,) * len(shape))
    grid_spec = pltpu.PrefetchScalarGridSpec(
        num_scalar_prefetch=1,
        grid=(bsz, nh),
        in_specs=[seq(dk, lambda h: h), seq(dk, lambda h: nh + h), seq(dv, lambda h: (2 * gk) // dv + h),
                  seq(dk, lambda h: h), seq(dk, lambda h: h),
                  st_spec, st_spec, cst(tree_f.shape), cst(tree_b.shape), cst(norm_g.shape)],
        out_specs=(seq(dv, lambda h: h), st_spec, st_spec),
        scratch_shapes=[pltpu.VMEM((t, dv), _F32), pltpu.VMEM((t, dv), _F32)],
    )
    return pl.pallas_call(
        functools.partial(_gla_kernel, chunk=c, scale=dk ** -0.5, unroll=min(t // c, 8)),
        out_shape=(jax.ShapeDtypeStruct((bsz, t, nh * dv), _BF16),
                   jax.ShapeDtypeStruct((bsz, nh, dv, dk), _F32), jax.ShapeDtypeStruct((bsz, nh, dv, dk), _F32)),
        grid_spec=grid_spec,
        compiler_params=_cparams(("parallel", "parallel")),
        name="gla",
    )(flags, p, p, p, cum_f, cum_b, s0_f, s0_b, tree_f, tree_b, norm_g)


def _mix_kernel(o_ref, g_ref, u_ref, x_ref, g1_ref, sh2_ref, sc2_ref,
                band_ref, invc_ref, wp_ref, ps_ref, wo_ref, lg_ref, lb_ref,
                xo_ref, h2_ref, mixed_scr, *, alpha, pool_dim):
    @pl.when(pl.program_id(0) == 0)
    def _():
        mixed_scr[...] = jnp.zeros_like(mixed_scr)

    xn = _ln_plain(alpha * x_ref[...] + g1_ref[...] * mixed_scr[...]) * lg_ref[...] + lb_ref[...]
    xo_ref[...] = xn
    h2_ref[...] = (_ln_plain(xn) * (1.0 + sc2_ref[...]) + sh2_ref[...]).astype(_BF16)

    tm = o_ref.shape[0]
    g = g_ref[...]
    gla = (o_ref[...].astype(_F32) * (g / (1.0 + jnp.exp(-g)))).astype(_BF16)
    pooled = []
    for t in range(tm // POOL_TILE):
        cols = []
        for gi in range(u_ref.shape[1] // pool_dim):
            ug = u_ref[t * POOL_TILE:(t + 1) * POOL_TILE, gi * pool_dim:(gi + 1) * pool_dim]
            u_hi, u_lo = _split2(ug)
            mean = (_dot(band_ref[gi], u_hi) + _dot(band_ref[gi], u_lo)) * invc_ref[gi]
            cols.append(_dot((mean - ug).astype(_BF16), wp_ref[gi]))
        pooled.append(jnp.concatenate(cols, axis=-1))
    pool = (jnp.concatenate(pooled, axis=0) * ps_ref[...]).astype(_BF16)
    v_w = gla.shape[1]
    mixed_scr[...] = _dot(gla, wo_ref[0:v_w, :]) + _dot(pool, wo_ref[v_w:, :])


def _mix(o_n, p, x, mod, layer, tile_mod_row, band, invc, w_pool, pool_scale, w_out, ln_g, ln_b, alpha):
    n, d = x.shape
    v = o_n.shape[-1]
    tm = POOL_TILE
    n_tiles = n // tm
    cur = lambda s: jnp.minimum(s, n_tiles - 1)
    prv = lambda s: jnp.maximum(s - 1, 0)
    row = lambda col: pl.BlockSpec((None, None, 1, d), lambda s: (layer, tile_mod_row(prv(s)), 0, col))
    lyr = lambda shape: pl.BlockSpec((None,) + shape, lambda s: (layer,) + (0,) * len(shape))
    cst = lambda a: pl.BlockSpec(a.shape, lambda s: (0,) * a.ndim)
    return pl.pallas_call(
        functools.partial(_mix_kernel, alpha=alpha, pool_dim=w_pool.shape[-1]),
        out_shape=(jax.ShapeDtypeStruct((n, d), _F32), jax.ShapeDtypeStruct((n, d), _BF16)),
        grid=(n_tiles + 1,),
        in_specs=[pl.BlockSpec((tm, v), lambda s: (cur(s), 0)),
                  pl.BlockSpec((tm, v), lambda s: (cur(s), 2)),
                  pl.BlockSpec((tm, v), lambda s: (cur(s), 3)),
                  pl.BlockSpec((tm, d), lambda s: (prv(s), 0)),
                  row(2), row(3), row(4),
                  cst(band), cst(invc), lyr(w_pool.shape[1:]), cst(pool_scale), lyr(w_out.shape[1:]),
                  cst(ln_g), cst(ln_b)],
        out_specs=(pl.BlockSpec((tm, d), lambda s: (prv(s), 0)), pl.BlockSpec((tm, d), lambda s: (prv(s), 0))),
        scratch_shapes=[pltpu.VMEM((tm, d), _F32)],
        compiler_params=_cparams(("arbitrary",)),
        name="mix",
    )(o_n, p, p, x, mod, mod, mod, band, invc, w_pool, pool_scale, w_out, ln_g, ln_b)


def _mlp_kernel(h_ref, w1_ref, b1_ref, w2_ref, b2_ref, x_ref, g2_ref, lg_ref, lb_ref, o_ref, acc_ref,
                *, alpha, n_tiles):
    i, f = pl.program_id(0), pl.program_id(1)

    def slab():
        a = jnp.maximum(_dot(h_ref[...], w1_ref[...]) + b1_ref[...], 0.0)
        return _dot((a * a).astype(_BF16), w2_ref[...])

    @pl.when((i == 0) & (f == 0))
    def _():
        acc_ref[...] = jnp.zeros_like(acc_ref)

    @pl.when(f == 0)
    def _():
        y = acc_ref[...] + b2_ref[...]
        o_ref[...] = _ln_plain(alpha * x_ref[...] + g2_ref[...] * y) * lg_ref[...] + lb_ref[...]
        acc_ref[...] = slab()

    @pl.when((f > 0) & (i < n_tiles))
    def _():
        acc_ref[...] += slab()


def _mlp(h2, x, mod, layer, tile_mod_row, tm, w1, b1, w2, b2, ln_g, ln_b, alpha):
    n, d = x.shape
    dff = w1.shape[-1]
    tf = 1024
    n_tiles, nf = n // tm, dff // tf
    cur = lambda i: jnp.minimum(i, n_tiles - 1)
    prv = lambda i: jnp.maximum(i - 1, 0)
    slab = lambda i, f: jnp.where(i < n_tiles, f, nf - 1)
    cst = lambda a: pl.BlockSpec(a.shape, lambda i, f: (0,) * a.ndim)
    return pl.pallas_call(
        functools.partial(_mlp_kernel, alpha=alpha, n_tiles=n_tiles),
        out_shape=jax.ShapeDtypeStruct((n, d), _F32),
        grid=(n_tiles + 1, nf),
        in_specs=[pl.BlockSpec((tm, d), lambda i, f: (cur(i), 0)),
                  pl.BlockSpec((None, d, tf), lambda i, f: (layer, 0, slab(i, f))),
                  pl.BlockSpec((1, tf), lambda i, f: (0, slab(i, f))),
                  pl.BlockSpec((None, tf, d), lambda i, f: (layer, slab(i, f), 0)),
                  cst(b2),
                  pl.BlockSpec((tm, d), lambda i, f: (prv(i), 0)),
                  pl.BlockSpec((None, None, 1, d), lambda i, f: (layer, tile_mod_row(prv(i)), 0, 5)),
                  cst(ln_g), cst(ln_b)],
        out_specs=pl.BlockSpec((tm, d), lambda i, f: (prv(i), 0)),
        scratch_shapes=[pltpu.VMEM((tm, d), _F32)],
        compiler_params=_cparams(("arbitrary", "arbitrary")),
        name="mlp",
    )(h2, w1, b1, w2, b2, x, mod, ln_g, ln_b)


def _pool_operators(seg_len):
    pos = np.arange(POOL_TILE)
    seg_start = (pos // seg_len) * seg_len
    seg_end = seg_start + seg_len - 1
    band = np.zeros((len(POOL_WINDOWS), POOL_TILE, POOL_TILE), np.float32)
    invc = np.zeros((len(POOL_WINDOWS), POOL_TILE, 2 * LANES), np.float32)
    for gi, w in enumerate(POOL_WINDOWS):
        lo = np.maximum(pos - w // 2, seg_start)
        hi = np.minimum(pos + w // 2 - 1, seg_end)
        for r in range(POOL_TILE):
            band[gi, r, lo[r]:hi[r] + 1] = 1.0
            invc[gi, r, :] = 1.0 / (hi[r] - lo[r] + 1)
    return jnp.asarray(band, _BF16), jnp.asarray(invc)


def _cumsum_operators(chunk, span):
    r = np.arange(span)
    same = (r[:, None] // chunk) == (r[None, :] // chunk)
    lo = same & (r[None, :] <= r[:, None])
    up = same & (r[None, :] >= r[:, None])
    return jnp.asarray(np.stack([lo, up]).astype(np.float32), _BF16)


def _tree_matrices(c, fwd):
    levels = int(math.log2(c))
    mats = np.zeros((levels, c, c), np.float32)
    for l in range(levels):
        m = 1 << l
        for r in range(c):
            mid = (r // (2 * m)) * 2 * m + m
            ref = mid - 1 if fwd else mid
            sign = 1.0 if (r >= mid) == fwd else -1.0
            mats[l, r, r] += sign
            mats[l, r, ref] -= sign
    return jnp.asarray(mats)


def kernel(x, c, ctx, c_ctx, w_ada, b_ada, w_in, w_gate_up, b_gate, gla_norm_g, w_pool, pool_scale, w_out,
           ln1_g, ln1_b, w_mlp1, b_mlp1, w_mlp2, b_mlp2, ln2_g, ln2_b):
    bsz, t, d = x.shape
    ctx_len = ctx.shape[1]
    depth = w_in.shape[0]
    gk = w_gate_up.shape[-1]
    dv = gla_norm_g.shape[-1]
    gv = dv * GLA_HEADS
    dk = gk // GLA_HEADS
    alpha = (2 * depth) ** 0.25
    chunk = GLA_CHUNK
    ctx_row = bsz

    cvecs8 = jnp.concatenate([c, c_ctx[None, :], jnp.zeros((8 - bsz - 1, d), _F32)], axis=0)
    mod = _ada(cvecs8, w_ada, b_ada[:, None, :]).reshape(depth, 8, 1, -1)

    w_p, w_lr = _cast_win(jnp.swapaxes(w_in, 1, 2), gk, gv)
    w_out_b = _cast_bf16(w_out, 512)
    w_pool_b = _cast_bf16(w_pool.reshape(depth, -1, w_pool.shape[-1]), 512).reshape(w_pool.shape)
    w1_b = _cast_bf16(w_mlp1, 128)
    w2_b = _cast_bf16(w_mlp2, 512)

    tri = _cumsum_operators(chunk, 2 * chunk)
    tree_f, tree_b = _tree_matrices(chunk, True), _tree_matrices(chunk, False)
    pool_lat, pool_ctx = _pool_operators(GRID_W), _pool_operators(ctx_len)
    zero_state = jnp.zeros((bsz, GLA_HEADS, dv, dk), _F32)
    zpad = jnp.zeros((GATE_RANK, gk), _F32)

    def flags(mx):
        m = mx.reshape(bsz, -1, 2, GLA_HEADS, dk).max(axis=(1, 2, 4))
        return (m > FAST_PATH_MAX_EXPONENT).astype(jnp.int32).reshape(-1)

    for l in range(depth):
        last = l == depth - 1
        w_g = jnp.concatenate([jnp.concatenate([w_gate_up[l, 0], zpad], axis=1),
                               jnp.concatenate([zpad, w_gate_up[l, 1]], axis=1)], axis=0)
        w_g_hi, w_g_lo = _split2(w_g)
        w_g = jnp.concatenate([w_g_hi, w_g_hi, w_g_lo], axis=0)
        b_g = jnp.concatenate([b_gate[l, 0], b_gate[l, 1]])[None, :]
        norm_g = gla_norm_g[l][None, :]
        pscale = pool_scale[l][None, :]
        b1, b2 = b_mlp1[l][None, :], b_mlp2[l][None, :]
        l1g, l1b, l2g, l2b = ln1_g[l][None, :], ln1_b[l][None, :], ln2_g[l][None, :], ln2_b[l][None, :]

        ctx_mod = lambda b: ctx_row
        lat_mod = lambda b: b

        def mixer(tokens, mod_row, s0_f, s0_b):
            p, cum_f, cum_b, mx = _inproj(tokens, mod, l, mod_row, w_p, w_lr, w_g, b_g, tri, chunk)
            o_n, s_f, s_b = _gla(p, cum_f, cum_b, flags(mx), s0_f, s0_b, tree_f, tree_b, norm_g)
            return p, o_n, s_f, s_b

        def rest(tokens, p, o_n, mod_row, pool_ops):
            n_tok = tokens.shape[1]
            flat = lambda a: a.reshape(bsz * n_tok, a.shape[-1])
            tile_row = lambda tm: (lambda i: mod_row(i // (n_tok // tm)))
            xm, h2 = _mix(flat(o_n), flat(p), flat(tokens), mod, l, tile_row(POOL_TILE), pool_ops[0], pool_ops[1],
                          w_pool_b, pscale, w_out_b, l1g, l1b, alpha)
            tm = 512 if (n_tok % 512 == 0 or mod_row is ctx_mod) else n_tok
            mlp_row = mod_row if mod_row is ctx_mod else tile_row(tm)
            return _mlp(h2, xm, mod, l, mlp_row, tm, w1_b, b1, w2_b, b2, l2g, l2b, alpha).reshape(bsz, n_tok, d)

        p_c, on_c, s_f, s_b = mixer(ctx, ctx_mod, zero_state, zero_state)
        if not last:
            ctx = rest(ctx, p_c, on_c, ctx_mod, pool_ctx)

        p_x, on_x, _, _ = mixer(x, lat_mod, s_f, s_b)
        x = rest(x, p_x, on_x, lat_mod, pool_lat)
    return x
```
